```python
import jax, jax.numpy as jnp
from jax import lax
import numpy as np

D_MODEL = 2048
BATCH = 1
SEQ = 8192
DEPTH = 1

CHUNK = 64
SGU_BLOCK = 128
D_A = 2048
G_A = 8
DG_A = D_A // G_A
D_B = 2048
CONV_W = 31
D_FF = 5632
PLE_DIM = 256
IN_COLS = 2 * D_A + 2 * D_B + 2 * D_MODEL
EPS = 1e-6

kernel_name = "hybrid_gmlp_conformer_conv_macaron_block"


def rmsnorm(x, g):
    xf = x.astype(jnp.float32)
    y = xf * lax.rsqrt(jnp.mean(xf * xf, axis=-1, keepdims=True) + EPS)
    return (y * g.astype(jnp.float32)).astype(x.dtype)


def layernorm(x, g, b):
    xf = x.astype(jnp.float32)
    mu = jnp.mean(xf, axis=-1, keepdims=True)
    var = jnp.mean(jnp.square(xf - mu), axis=-1, keepdims=True)
    y = (xf - mu) * lax.rsqrt(var + EPS)
    return (y * g.astype(jnp.float32) + b.astype(jnp.float32)).astype(x.dtype)


def swiglu(x, w_gate, w_up, w_down):
    return (jax.nn.silu(x @ w_gate) * (x @ w_up)) @ w_down


def spatial_gating(u, v, ln_g, ln_b, w_s, b_s):
    bsz, seq, _ = v.shape
    nb = seq // SGU_BLOCK
    v = layernorm(v, ln_g, ln_b)
    idx = jnp.arange(SGU_BLOCK)
    allowed = (idx[None, :] // CHUNK) <= (idx[:, None] // CHUNK)
    w_m = jnp.where(allowed[None], w_s, jnp.zeros_like(w_s))
    vb = v.reshape(bsz, nb, SGU_BLOCK, G_A, DG_A)
    s = jnp.einsum('gij,bnjgc->bnigc', w_m, vb) + b_s.T[None, None, :, :, None]
    return u * s.reshape(bsz, seq, D_A)


def conformer_conv(a, b, dw_w, dw_b, ln_g, ln_b):
    z = a * jax.nn.sigmoid(b)
    y = lax.conv_general_dilated(
        z, dw_w[:, None, :].astype(z.dtype), window_strides=(1,),
        padding=[(CONV_W - 1, 0)], dimension_numbers=('NWC', 'WIO', 'NWC'),
        feature_group_count=D_B) + dw_b
    return jax.nn.silu(layernorm(y, ln_g, ln_b))


def setup_inputs(seed: int = 0) -> dict:
    key = jax.random.key(seed)
    ks = iter(jax.random.split(key, 40))
    L = DEPTH

    def w(shape, fan_in):
        return jax.random.normal(next(ks), shape, jnp.float32) * fan_in ** -0.5

    def gain(shape):
        return 1.0 + 0.05 * jax.random.normal(next(ks), shape, jnp.float32)

    def bias(shape):
        return 0.02 * jax.random.normal(next(ks), shape, jnp.float32)

    return {
        "x": jax.random.normal(next(ks), (BATCH, SEQ, D_MODEL), jnp.float32),
        "p": jax.random.normal(next(ks), (DEPTH, BATCH, SEQ, PLE_DIM), jnp.float32),
        "ffn1_norm": gain((L, D_MODEL)),
        "ffn1_w_gate": w((L, D_MODEL, D_FF), D_MODEL),
        "ffn1_w_up": w((L, D_MODEL, D_FF), D_MODEL),
        "ffn1_w_down": w((L, D_FF, D_MODEL), D_FF),
        "mix_norm": gain((L, D_MODEL)),
        "w_in": w((L, D_MODEL, IN_COLS), D_MODEL),
        "sgu_ln_g": gain((L, D_A)),
        "sgu_ln_b": bias((L, D_A)),
        "sgu_w": w((L, G_A, SGU_BLOCK, SGU_BLOCK), SGU_BLOCK) * 0.5,
        "sgu_b": 1.0 + 0.05 * jax.random.normal(next(ks), (L, G_A, SGU_BLOCK), jnp.float32),
        "w_a_proj": w((L, D_A, D_MODEL), D_A),
        "dw_w": w((L, CONV_W, D_B), CONV_W),
        "dw_b": bias((L, D_B)),
        "conv_ln_g": gain((L, D_B)),
        "conv_ln_b": bias((L, D_B)),
        "w_b_proj": w((L, D_B, D_MODEL), D_B),
        "w_out": w((L, D_MODEL, D_MODEL), D_MODEL),
        "ffn2_norm": gain((L, D_MODEL)),
        "ffn2_w_gate": w((L, D_MODEL, D_FF), D_MODEL),
        "ffn2_w_up": w((L, D_MODEL, D_FF), D_MODEL),
        "ffn2_w_down": w((L, D_FF, D_MODEL), D_FF),
        "ple_norm": gain((L, D_MODEL)),
        "w_ple_gate": w((L, D_MODEL, D_MODEL), D_MODEL),
        "w_ple_proj": w((L, PLE_DIM, D_MODEL), PLE_DIM),
        "final_norm": gain((D_MODEL,)),
    }


def reference(x, p, ffn1_norm, ffn1_w_gate, ffn1_w_up, ffn1_w_down, mix_norm, w_in,
              sgu_ln_g, sgu_ln_b, sgu_w, sgu_b, w_a_proj, dw_w, dw_b, conv_ln_g,
              conv_ln_b, w_b_proj, w_out, ffn2_norm, ffn2_w_gate, ffn2_w_up,
              ffn2_w_down, ple_norm, w_ple_gate, w_ple_proj, final_norm):
    h = x
    for i in range(DEPTH):
        h = h + 0.5 * swiglu(rmsnorm(h, ffn1_norm[i]), ffn1_w_gate[i], ffn1_w_up[i], ffn1_w_down[i])

        n = rmsnorm(h, mix_norm[i])
        z = n @ w_in[i]
        o1 = D_A
        o2 = o1 + D_A
        o3 = o2 + D_B
        o4 = o3 + D_B
        o5 = o4 + D_MODEL
        u_a, v_a = z[..., :o1], z[..., o1:o2]
        glu_a, glu_b = z[..., o2:o3], z[..., o3:o4]
        gate_a, gate_b = z[..., o4:o5], z[..., o5:]

        y_a = spatial_gating(u_a, v_a, sgu_ln_g[i], sgu_ln_b[i], sgu_w[i], sgu_b[i]) @ w_a_proj[i]
        y_b = conformer_conv(glu_a, glu_b, dw_w[i], dw_b[i], conv_ln_g[i], conv_ln_b[i]) @ w_b_proj[i]

        m = jax.nn.sigmoid(gate_a) * y_a + jax.nn.sigmoid(gate_b) * y_b
        h = h + m @ w_out[i]

        h = h + 0.5 * swiglu(rmsnorm(h, ffn2_norm[i]), ffn2_w_gate[i], ffn2_w_up[i], ffn2_w_down[i])

        g = jax.nn.sigmoid(rmsnorm(h, ple_norm[i]) @ w_ple_gate[i])
        h = h + g * (p[i] @ w_ple_proj[i])
    return rmsnorm(h, final_norm)
```

```python
import functools

import jax
import jax.numpy as jnp
from jax import lax
from jax.experimental import pallas as pl
from jax.experimental.pallas import tpu as pltpu

EPS = 1e-6
CHUNK = 64
SGU_BLOCK = 128
CONV_W = 31
HALO = 32

F32 = jnp.float32
BF16 = jnp.bfloat16

VMEM_LIMIT_BYTES = 60 * 1024 * 1024


def _params(n_axes):
    return pltpu.CompilerParams(
        dimension_semantics=("arbitrary",) * n_axes,
        vmem_limit_bytes=VMEM_LIMIT_BYTES,
    )


def _rmsnorm(x, g):
    return x * lax.rsqrt(jnp.mean(x * x, axis=-1, keepdims=True) + EPS) * g


def _sigmoid(x):
    return 1.0 / (1.0 + jnp.exp(-x))


def _dot(a, b):
    return jnp.dot(a, b, preferred_element_type=F32)


def _ffn_kernel(x_ref, g_ref, wg_ref, wu_ref, wd_ref, o_ref, n_ref):
    f = pl.program_id(1)

    @pl.when(f == 0)
    def _():
        x = x_ref[...]
        n_ref[...] = _rmsnorm(x, g_ref[...]).astype(BF16)
        o_ref[...] = x

    n = n_ref[...]
    gate = _dot(n, wg_ref[...].astype(BF16))
    up = _dot(n, wu_ref[...].astype(BF16))
    a = (0.5 * gate * _sigmoid(gate) * up).astype(BF16)
    o_ref[...] += _dot(a, wd_ref[...].astype(BF16))


def _ffn(x, gain, wg, wu, wd, *, tm, tf):
    m, d = x.shape
    dff = wg.shape[1]
    return pl.pallas_call(
        _ffn_kernel,
        grid=(m // tm, dff // tf),
        in_specs=[
            pl.BlockSpec((tm, d), lambda i, f: (i, 0)),
            pl.BlockSpec((1, d), lambda i, f: (0, 0)),
            pl.BlockSpec((d, tf), lambda i, f: (0, f)),
            pl.BlockSpec((d, tf), lambda i, f: (0, f)),
            pl.BlockSpec((tf, d), lambda i, f: (f, 0)),
        ],
        out_specs=pl.BlockSpec((tm, d), lambda i, f: (i, 0)),
        out_shape=jax.ShapeDtypeStruct((m, d), F32),
        scratch_shapes=[pltpu.VMEM((tm, d), BF16)],
        compiler_params=_params(2),
        name="ffn",
    )(x, gain, wg, wu, wd)


def _sgu_kernel(h_ref, g_ref, w_ref, lng_ref, lnb_ref, ws_ref, bs_ref, o_ref,
                n_ref, u_ref, v_ref, *, n_col, tn, dg):
    j = pl.program_id(1)
    tm = h_ref.shape[0]

    @pl.when(j == 0)
    def _():
        n_ref[...] = _rmsnorm(h_ref[...], g_ref[...]).astype(BF16)

    z = _dot(n_ref[...], w_ref[...])

    @pl.when(j < n_col)
    def _():
        u_ref[j] = z

    @pl.when(j >= n_col)
    def _():
        v_ref[j - n_col] = z

    @pl.when(j == 2 * n_col - 1)
    def _():
        d_a = n_col * tn
        s1 = jnp.zeros((tm, 1), F32)
        for t in range(n_col):
            s1 = s1 + jnp.sum(v_ref[t], axis=-1, keepdims=True)
        mu = s1 / d_a
        s2 = jnp.zeros((tm, 1), F32)
        for t in range(n_col):
            c = v_ref[t] - mu
            s2 = s2 + jnp.sum(c * c, axis=-1, keepdims=True)
        rstd = lax.rsqrt(s2 / d_a + EPS)

        row = lax.broadcasted_iota(jnp.int32, (SGU_BLOCK, SGU_BLOCK), 0)
        col = lax.broadcasted_iota(jnp.int32, (SGU_BLOCK, SGU_BLOCK), 1)
        allowed = (col // CHUNK) <= (row // CHUNK)

        groups_per_tile = tn // dg
        for t in range(n_col):
            cs = slice(t * tn, (t + 1) * tn)
            vln = ((v_ref[t] - mu) * rstd * lng_ref[:, cs] + lnb_ref[:, cs]).astype(BF16)
            for gi in range(groups_per_tile):
                g = t * groups_per_tile + gi
                wm = jnp.where(allowed, ws_ref[g], 0.0).astype(BF16)
                lanes = slice(gi * dg, (gi + 1) * dg)
                for b in range(tm // SGU_BLOCK):
                    rows = slice(b * SGU_BLOCK, (b + 1) * SGU_BLOCK)
                    s = _dot(wm, vln[rows, lanes]) + bs_ref[g]
                    o_ref[rows, t * tn + gi * dg:t * tn + (gi + 1) * dg] = (
                        u_ref[t, rows, lanes] * s).astype(BF16)


def _sgu_branch(h, gain, w_in, ln_g, ln_b, ws, bs_b, *, tm, tn, d_a):
    m, d = h.shape
    n_col = d_a // tn
    g_a = ws.shape[0]
    dg = d_a // g_a
    kern = functools.partial(_sgu_kernel, n_col=n_col, tn=tn, dg=dg)
    return pl.pallas_call(
        kern,
        grid=(m // tm, 2 * n_col),
        in_specs=[
            pl.BlockSpec((tm, d), lambda i, j: (i, 0)),
            pl.BlockSpec((1, d), lambda i, j: (0, 0)),
            pl.BlockSpec((d, tn), lambda i, j: (0, j)),
            pl.BlockSpec((1, d_a), lambda i, j: (0, 0)),
            pl.BlockSpec((1, d_a), lambda i, j: (0, 0)),
            pl.BlockSpec(ws.shape, lambda i, j: (0, 0, 0)),
            pl.BlockSpec(bs_b.shape, lambda i, j: (0, 0, 0)),
        ],
        out_specs=pl.BlockSpec((tm, d_a), lambda i, j: (i, 0)),
        out_shape=jax.ShapeDtypeStruct((m, d_a), BF16),
        scratch_shapes=[
            pltpu.VMEM((tm, d), BF16),
            pltpu.VMEM((n_col, tm, tn), F32),
            pltpu.VMEM((n_col, tm, tn), F32),
        ],
        compiler_params=_params(2),
        name="sgu_branch",
    )(h, gain, w_in, ln_g, ln_b, ws, bs_b)


def _conv_kernel(h_ref, g_ref, wa_ref, wb_ref, dw_ref, dwb_ref, lng_ref, lnb_ref, o_ref,
                 n_ref, zbuf_ref, halo_ref, y_ref, *, n_col, tn, rows_per_chunk):
    i = pl.program_id(0)
    j = pl.program_id(1)
    tm = h_ref.shape[0]

    @pl.when(j == 0)
    def _():
        n_ref[...] = _rmsnorm(h_ref[...], g_ref[...]).astype(BF16)

    n = n_ref[...]
    zz = _dot(n, wa_ref[...]) * _sigmoid(_dot(n, wb_ref[...]))

    @pl.when(i == 0)
    def _():
        zbuf_ref[0:HALO, :] = jnp.zeros((HALO, tn), F32)

    @pl.when(i > 0)
    def _():
        zbuf_ref[0:HALO, :] = halo_ref[j]

    zbuf_ref[HALO:HALO + tm, :] = zz
    halo_ref[j] = zz[tm - HALO:tm, :]

    first = HALO - (CONV_W - 1)
    rc = rows_per_chunk

    def chunk(ci, carry):
        base = pl.multiple_of(ci * rc, rc)
        for c in range(tn // 128):
            lanes = slice(c * 128, (c + 1) * 128)
            win = zbuf_ref[pl.ds(base, rc + HALO), lanes]
            acc = jnp.zeros((rc, 128), F32)
            for shift in range(8):
                taps = [k for k in range(CONV_W) if (first + k) % 8 == shift]
                rows_needed = max(first + k for k in taps) - shift + rc
                shifted = win[shift:shift + rows_needed]
                for k in taps:
                    off = first + k - shift
                    tap = jnp.concatenate([dw_ref[k, :, lanes]] * (rc // 8), axis=0)
                    acc = acc + tap * shifted[off:off + rc]
            y_ref[j, pl.ds(base, rc), lanes] = acc + dwb_ref[:, lanes]
        return carry

    lax.fori_loop(0, tm // rows_per_chunk, chunk, 0)

    @pl.when(j == n_col - 1)
    def _():
        d_b = n_col * tn
        s1 = jnp.zeros((tm, 1), F32)
        for t in range(n_col):
            s1 = s1 + jnp.sum(y_ref[t], axis=-1, keepdims=True)
        mu = s1 / d_b
        s2 = jnp.zeros((tm, 1), F32)
        for t in range(n_col):
            c = y_ref[t] - mu
            s2 = s2 + jnp.sum(c * c, axis=-1, keepdims=True)
        rstd = lax.rsqrt(s2 / d_b + EPS)
        for t in range(n_col):
            cs = slice(t * tn, (t + 1) * tn)
            yn = (y_ref[t] - mu) * rstd * lng_ref[:, cs] + lnb_ref[:, cs]
            o_ref[:, cs] = (yn * _sigmoid(yn)).astype(BF16)


def _conv_branch(h, gain, w_in, dw_b8, dw_bias, ln_g, ln_b, *, tm, tn, col_a, col_b, d_b):
    m, d = h.shape
    n_col = d_b // tn
    ja, jb = col_a // tn, col_b // tn
    kern = functools.partial(_conv_kernel, n_col=n_col, tn=tn, rows_per_chunk=32)
    return pl.pallas_call(
        kern,
        grid=(m // tm, n_col),
        in_specs=[
            pl.BlockSpec((tm, d), lambda i, j: (i, 0)),
            pl.BlockSpec((1, d), lambda i, j: (0, 0)),
            pl.BlockSpec((d, tn), lambda i, j: (0, ja + j)),
            pl.BlockSpec((d, tn), lambda i, j: (0, jb + j)),
            pl.BlockSpec((CONV_W, 8, tn), lambda i, j: (0, 0, j)),
            pl.BlockSpec((1, tn), lambda i, j: (0, j)),
            pl.BlockSpec((1, d_b), lambda i, j: (0, 0)),
            pl.BlockSpec((1, d_b), lambda i, j: (0, 0)),
        ],
        out_specs=pl.BlockSpec((tm, d_b), lambda i, j: (i, 0)),
        out_shape=jax.ShapeDtypeStruct((m, d_b), BF16),
        scratch_shapes=[
            pltpu.VMEM((tm, d), BF16),
            pltpu.VMEM((HALO + tm, tn), F32),
            pltpu.VMEM((n_col, HALO, tn), F32),
            pltpu.VMEM((n_col, tm, tn), F32),
        ],
        compiler_params=_params(2),
        name="conv_branch",
    )(h, gain, w_in, w_in, dw_b8, dw_bias, ln_g, ln_b)


def _gate_kernel(h_ref, g_ref, wga_ref, wgb_ref, ya_ref, yb_ref, wa_ref, wb_ref, o_ref, n_ref):
    j = pl.program_id(1)

    @pl.when(j == 0)
    def _():
        n_ref[...] = _rmsnorm(h_ref[...], g_ref[...]).astype(BF16)

    n = n_ref[...]
    ga = _sigmoid(_dot(n, wga_ref[...]))
    gb = _sigmoid(_dot(n, wgb_ref[...]))
    pa = _dot(ya_ref[...], wa_ref[...])
    pb = _dot(yb_ref[...], wb_ref[...])
    o_ref[...] = (ga * pa + gb * pb).astype(BF16)


def _gated_mix(h, gain, w_in, ya, yb, wa, wb, *, tm, tn, col_ga, col_gb):
    m, d = h.shape
    jga, jgb = col_ga // tn, col_gb // tn
    return pl.pallas_call(
        _gate_kernel,
        grid=(m // tm, d // tn),
        in_specs=[
            pl.BlockSpec((tm, d), lambda i, j: (i, 0)),
            pl.BlockSpec((1, d), lambda i, j: (0, 0)),
            pl.BlockSpec((d, tn), lambda i, j: (0, jga + j)),
            pl.BlockSpec((d, tn), lambda i, j: (0, jgb + j)),
            pl.BlockSpec((tm, ya.shape[1]), lambda i, j: (i, 0)),
            pl.BlockSpec((tm, yb.shape[1]), lambda i, j: (i, 0)),
            pl.BlockSpec((wa.shape[0], tn), lambda i, j: (0, j)),
            pl.BlockSpec((wb.shape[0], tn), lambda i, j: (0, j)),
        ],
        out_specs=pl.BlockSpec((tm, tn), lambda i, j: (i, j)),
        out_shape=jax.ShapeDtypeStruct((m, d), BF16),
        scratch_shapes=[pltpu.VMEM((tm, d), BF16)],
        compiler_params=_params(2),
        name="gated_mix",
    )(h, gain, w_in, w_in, ya, yb, wa, wb)


def _resident(shape):
    return pl.BlockSpec(shape, lambda i: (0,) * len(shape), pipeline_mode=pl.Buffered(1))


def _out_proj_kernel(h_ref, m_ref, w_ref, o_ref, wbf_ref):
    @pl.when(pl.program_id(0) == 0)
    def _():
        wbf_ref[...] = w_ref[...].astype(BF16)

    o_ref[...] = h_ref[...] + _dot(m_ref[...], wbf_ref[...])


def _out_proj(h, mix, w, *, tm):
    m, d = h.shape
    return pl.pallas_call(
        _out_proj_kernel,
        grid=(m // tm,),
        in_specs=[
            pl.BlockSpec((tm, d), lambda i: (i, 0)),
            pl.BlockSpec((tm, mix.shape[1]), lambda i: (i, 0)),
            _resident(w.shape),
        ],
        out_specs=pl.BlockSpec((tm, d), lambda i: (i, 0)),
        out_shape=jax.ShapeDtypeStruct((m, d), F32),
        scratch_shapes=[pltpu.VMEM(w.shape, BF16)],
        compiler_params=_params(1),
        name="out_proj",
    )(h, mix, w)


def _ple_kernel(h_ref, p_ref, gp_ref, wpg_ref, wpp_ref, gf_ref, o_ref, wpg_bf_ref, wpp_bf_ref):
    @pl.when(pl.program_id(0) == 0)
    def _():
        wpg_bf_ref[...] = wpg_ref[...].astype(BF16)
        wpp_bf_ref[...] = wpp_ref[...].astype(BF16)

    h = h_ref[...]
    gate = _sigmoid(_dot(_rmsnorm(h, gp_ref[...]).astype(BF16), wpg_bf_ref[...]))
    emb = _dot(p_ref[...].astype(BF16), wpp_bf_ref[...])
    o_ref[...] = _rmsnorm(h + gate * emb, gf_ref[...])


def _ple_final(h, p, gain_p, wpg, wpp, gain_f, *, tm):
    m, d = h.shape
    dp = p.shape[1]
    return pl.pallas_call(
        _ple_kernel,
        grid=(m // tm,),
        in_specs=[
            pl.BlockSpec((tm, d), lambda i: (i, 0)),
            pl.BlockSpec((tm, dp), lambda i: (i, 0)),
            pl.BlockSpec((1, d), lambda i: (0, 0)),
            _resident(wpg.shape),
            _resident(wpp.shape),
            pl.BlockSpec((1, d), lambda i: (0, 0)),
        ],
        out_specs=pl.BlockSpec((tm, d), lambda i: (i, 0)),
        out_shape=jax.ShapeDtypeStruct((m, d), F32),
        scratch_shapes=[pltpu.VMEM(wpg.shape, BF16), pltpu.VMEM(wpp.shape, BF16)],
        compiler_params=_params(1),
        name="ple_final",
    )(h, p, gain_p, wpg, wpp, gain_f)


def kernel(x, p, ffn1_norm, ffn1_w_gate, ffn1_w_up, ffn1_w_down, mix_norm, w_in, sgu_ln_g, sgu_ln_b, sgu_w, sgu_b, w_a_proj, dw_w, dw_b, conv_ln_g, conv_ln_b, w_b_proj, w_out, ffn2_norm, ffn2_w_gate, ffn2_w_up, ffn2_w_down, ple_norm, w_ple_gate, w_ple_proj, final_norm):
    bsz, seq, d = x.shape
    depth = w_in.shape[0]
    d_a = sgu_ln_g.shape[-1]
    d_b = dw_b.shape[-1]
    g_a = sgu_w.shape[1]
    dg = d_a // g_a

    h = x.reshape(bsz * seq, d)
    assert bsz == 1, "row tiles must not straddle batch entries (conv history, position blocks)"
    assert depth == 1, "the final RMSNorm is fused into the (single) layer's embedding step"

    for l in range(depth):
        bf = lambda w: w[l].astype(BF16)
        row = lambda v: v[l].reshape(1, -1)

        h = _ffn(h, row(ffn1_norm), ffn1_w_gate[l], ffn1_w_up[l], ffn1_w_down[l], tm=1024, tf=256)

        w_in_l = bf(w_in)
        bs_b = jnp.broadcast_to(sgu_b[l][:, :, None], (g_a, SGU_BLOCK, dg))
        ya = _sgu_branch(h, row(mix_norm), w_in_l, row(sgu_ln_g), row(sgu_ln_b), sgu_w[l], bs_b,
                         tm=512, tn=512, d_a=d_a)
        dw_b8 = jnp.broadcast_to(dw_w[l][:, None, :], (CONV_W, 8, d_b))
        yb = _conv_branch(h, row(mix_norm), w_in_l, dw_b8, row(dw_b), row(conv_ln_g), row(conv_ln_b),
                          tm=512, tn=512, col_a=2 * d_a, col_b=2 * d_a + d_b, d_b=d_b)
        mix = _gated_mix(h, row(mix_norm), w_in_l, ya, yb, bf(w_a_proj), bf(w_b_proj),
                         tm=512, tn=512, col_ga=2 * d_a + 2 * d_b, col_gb=2 * d_a + 2 * d_b + d)
        h = _out_proj(h, mix, w_out[l], tm=512)

        h = _ffn(h, row(ffn2_norm), ffn2_w_gate[l], ffn2_w_up[l], ffn2_w_down[l], tm=1024, tf=256)

        h = _ple_final(h, p[l].reshape(bsz * seq, -1), row(ple_norm), w_ple_gate[l], w_ple_proj[l],
                       final_norm.reshape(1, -1), tm=512)
    return h.reshape(bsz, seq, d)
```
